```python
import math
import jax
import jax.numpy as jnp
from jax import lax
import numpy as np


D_MODEL = 1024
BATCH = 16
SEQ = 4096
DEPTH = 1

GRID_W = 64
CTX_LEN = 256
ATT_HEADS = D_MODEL // 256
ATT_QK_DIM = 64
ATT_V_DIM = 2 * ATT_QK_DIM
ATT_WIDTH = ATT_HEADS * ATT_V_DIM
ATT_QK_WIDTH = ATT_HEADS * 2 * ATT_QK_DIM
LRU_WIDTH = D_MODEL - ATT_WIDTH
LRU_BLOCKS = 8
LRU_BLOCK_DIM = LRU_WIDTH // LRU_BLOCKS
LRU_C = 8.0
CONV_W = 4
CONV_LEFT = CONV_W // 2
IN_WIDTH = 2 * ATT_QK_WIDTH + ATT_WIDTH + 2 * LRU_WIDTH
FFN_HIDDEN = int(math.ceil(8 * D_MODEL / 3 / 256)) * 256
N_MOD = 6
Q_BLOCK = 128
ROPE_BASE = 10000.0
ROPE_FREQS = ATT_QK_DIM // 4
ATT_SCALE = ATT_QK_DIM ** -0.5
DEEPNORM_ALPHA = (2 * DEPTH) ** 0.25
DEEPNORM_BETA = (8 * DEPTH) ** -0.25
LN_EPS = 1e-6
RMS_EPS = 1e-5

kernel_name = 'hybrid_diffattn_rglru_block'


def layer_norm(x, g=None, b=None):
    xf = x.astype(jnp.float32)
    mu = jnp.mean(xf, axis=-1, keepdims=True)
    var = jnp.mean(jnp.square(xf - mu), axis=-1, keepdims=True)
    y = (xf - mu) * lax.rsqrt(var + LN_EPS)
    if g is not None:
        y = y * g + b
    return y.astype(x.dtype)


def modulate(h, shift, scale):
    return h * (1 + scale) + shift


def axial_rope_tables(n_tokens):
    rows = n_tokens // GRID_W
    row = jnp.repeat(jnp.arange(rows, dtype=jnp.float32), GRID_W)
    col = jnp.tile(jnp.arange(GRID_W, dtype=jnp.float32), rows)
    inv_freq = ROPE_BASE ** (-jnp.arange(ROPE_FREQS, dtype=jnp.float32) / ROPE_FREQS)
    ang = jnp.stack([row[:, None] * inv_freq, col[:, None] * inv_freq], axis=1)
    return jnp.cos(ang), jnp.sin(ang)


def apply_rope(t, cos, sin):
    tr = t.astype(jnp.float32).reshape(t.shape[:-1] + (2, 2, ROPE_FREQS))
    t1, t2 = tr[..., 0, :], tr[..., 1, :]
    cs, sn = cos[:, None, None], sin[:, None, None]
    out = jnp.stack([t1 * cs - t2 * sn, t2 * cs + t1 * sn], axis=-2)
    return out.reshape(t.shape).astype(t.dtype)


def split_proj(p):
    B, S, _ = p.shape
    q, k, v, u, g = jnp.split(p, [ATT_QK_WIDTH, 2 * ATT_QK_WIDTH, 2 * ATT_QK_WIDTH + ATT_WIDTH,
                                  2 * ATT_QK_WIDTH + ATT_WIDTH + LRU_WIDTH], axis=-1)
    q = q.reshape(B, S, ATT_HEADS, 2, ATT_QK_DIM)
    k = k.reshape(B, S, ATT_HEADS, 2, ATT_QK_DIM)
    v = v.reshape(B, S, ATT_HEADS, ATT_V_DIM)
    return q, k, v, u, g


def diff_attention(q, k_all, v_all, lam):
    B, S, H = q.shape[:3]
    nb = S // Q_BLOCK
    qb = jnp.moveaxis(q.reshape(B, nb, Q_BLOCK, H, 2, ATT_QK_DIM), 1, 0)

    def one_block(q_blk):
        s = jnp.einsum('bqhmd,bkhmd->bmhqk', q_blk, k_all).astype(jnp.float32) * ATT_SCALE
        p = jax.nn.softmax(s, axis=-1)
        a = p[:, 0] - lam * p[:, 1]
        return jnp.einsum('bhqk,bkhd->bqhd', a.astype(v_all.dtype), v_all)

    o = lax.map(one_block, qb)
    return jnp.moveaxis(o, 0, 1).reshape(B, S, H, ATT_V_DIM)


def diff_head_norm(o, g, lam_init):
    B, S = o.shape[:2]
    of = o.astype(jnp.float32)
    y = of * lax.rsqrt(jnp.mean(jnp.square(of), axis=-1, keepdims=True) + RMS_EPS) * g * (1.0 - lam_init)
    return y.astype(o.dtype).reshape(B, S, ATT_WIDTH)


def centred_conv(u, w, b):
    S = u.shape[1]
    up = jnp.pad(u, ((0, 0), (CONV_LEFT, CONV_W - 1 - CONV_LEFT), (0, 0)))
    out = up[:, 0:S] * w[0]
    for j in range(1, CONV_W):
        out = out + up[:, j:j + S] * w[j]
    return out + b


def rglru_coeffs(u, w_gates, b_gates, lam):
    B, S, _ = u.shape
    ub = u.reshape(B, S, LRU_BLOCKS, LRU_BLOCK_DIM)
    z = jnp.einsum('bshi,ghij->gbshj', ub, w_gates) + b_gates[:, None, None]
    gates = jax.nn.sigmoid(z.astype(jnp.float32)).reshape(2, B, S, LRU_WIDTH)
    r, i = gates[0], gates[1]
    log_a = -LRU_C * r * jax.nn.softplus(-lam.astype(jnp.float32))
    a = jnp.exp(log_a)
    mult = jnp.sqrt(-jnp.expm1(2.0 * log_a))
    return a, mult * i * u.astype(jnp.float32)


def linear_scan(a, b, h0, reverse):
    def combine(e1, e2):
        a1, b1 = e1
        a2, b2 = e2
        return a1 * a2, a2 * b1 + b2
    a_cum, h = lax.associative_scan(combine, (a, b), axis=1, reverse=reverse)
    return h + a_cum * h0[:, None]


def mixing_sublayer(h, hc, cos, sin, w_in, lam_q, lam_k, lam_init, subln_g, conv_w, conv_b,
                    w_gates, b_gates, lru_lambda, w_out, ctx_out):
    q, k, v, u, g = split_proj(h @ w_in)
    qc, kc, vc, uc, gc = split_proj(hc @ w_in)
    q = apply_rope(q, cos, sin)
    k = apply_rope(k, cos, sin)
    k_all = jnp.concatenate([k, kc], axis=1)
    v_all = jnp.concatenate([v, vc], axis=1)
    lq = jnp.sum(lam_q.astype(jnp.float32) * lam_k.astype(jnp.float32), axis=-1)
    lam = jnp.exp(lq[0]) - jnp.exp(lq[1]) + lam_init
    o = diff_head_norm(diff_attention(q, k_all, v_all, lam), subln_g, lam_init)
    u_conv = centred_conv(u, conv_w, conv_b)
    uc_conv = centred_conv(uc, conv_w, conv_b)
    zero_state = jnp.zeros((uc.shape[0], LRU_WIDTH), jnp.float32)
    lat_dirs, ctx_dirs = [], []
    for d in range(2):
        rev = d == 1
        a_c, b_c = rglru_coeffs(uc_conv, w_gates[d], b_gates[d], lru_lambda[d])
        hs_c = linear_scan(a_c, b_c, zero_state, rev)
        state = hs_c[:, 0] if rev else hs_c[:, -1]
        a_l, b_l = rglru_coeffs(u_conv, w_gates[d], b_gates[d], lru_lambda[d])
        lat_dirs.append(linear_scan(a_l, b_l, state, rev))
        ctx_dirs.append(hs_c)
    y_lru = (lat_dirs[0] + lat_dirs[1]).astype(u.dtype) * jax.nn.gelu(g)
    y = jnp.concatenate([o, y_lru], axis=-1) @ w_out
    if not ctx_out:
        return y, None
    oc = diff_head_norm(diff_attention(qc, kc, vc, lam), subln_g, lam_init)
    yc_lru = (ctx_dirs[0] + ctx_dirs[1]).astype(uc.dtype) * jax.nn.gelu(gc)
    yc = jnp.concatenate([oc, yc_lru], axis=-1) @ w_out
    return y, yc


def swiglu(h, w1, w2):
    a, b = jnp.split(h @ w1, 2, axis=-1)
    return (jax.nn.silu(a) * b) @ w2


def setup_inputs(seed: int = 0) -> dict:
    key = jax.random.key(seed)
    ks = jax.random.split(key, 22)

    def nrm(k, shape, s):
        return jax.random.normal(k, shape, jnp.float32) * s

    u = jax.random.uniform(ks[14], (DEPTH, 2, LRU_WIDTH), jnp.float32, minval=0.9, maxval=0.999)
    a0 = u ** (1.0 / LRU_C)
    lru_lambda = jnp.log(a0) - jnp.log1p(-a0)
    return {
        'x': nrm(ks[0], (BATCH, SEQ, D_MODEL), 1.0),
        'c': nrm(ks[1], (BATCH, D_MODEL), 1.0),
        'ctx': nrm(ks[2], (BATCH, CTX_LEN, D_MODEL), 1.0),
        'c_ctx': nrm(ks[3], (D_MODEL,), 1.0),
        'w_ada': nrm(ks[4], (DEPTH, D_MODEL, N_MOD * D_MODEL), 0.5 * D_MODEL ** -0.5),
        'b_ada': nrm(ks[5], (DEPTH, N_MOD * D_MODEL), 0.02),
        'w_in': nrm(ks[6], (DEPTH, D_MODEL, IN_WIDTH), D_MODEL ** -0.5),
        'lam_q': nrm(ks[7], (DEPTH, 2, ATT_QK_DIM), 0.1),
        'lam_k': nrm(ks[8], (DEPTH, 2, ATT_QK_DIM), 0.1),
        'subln_g': 1.0 + nrm(ks[9], (DEPTH, ATT_V_DIM), 0.02),
        'conv_w': nrm(ks[10], (DEPTH, CONV_W, LRU_WIDTH), CONV_W ** -0.5),
        'conv_b': nrm(ks[11], (DEPTH, LRU_WIDTH), 0.02),
        'lru_w_gates': nrm(ks[12], (DEPTH, 2, 2, LRU_BLOCKS, LRU_BLOCK_DIM, LRU_BLOCK_DIM), LRU_BLOCK_DIM ** -0.5),
        'lru_b_gates': nrm(ks[13], (DEPTH, 2, 2, LRU_BLOCKS, LRU_BLOCK_DIM), 0.02),
        'lru_lambda': lru_lambda,
        'w_out': nrm(ks[15], (DEPTH, D_MODEL, D_MODEL), DEEPNORM_BETA * D_MODEL ** -0.5),
        'ln1_g': 1.0 + nrm(ks[16], (DEPTH, D_MODEL), 0.02),
        'ln1_b': nrm(ks[17], (DEPTH, D_MODEL), 0.02),
        'w_ffn_in': nrm(ks[18], (DEPTH, D_MODEL, 2 * FFN_HIDDEN), D_MODEL ** -0.5),
        'w_ffn_out': nrm(ks[19], (DEPTH, FFN_HIDDEN, D_MODEL), DEEPNORM_BETA * FFN_HIDDEN ** -0.5),
        'ln2_g': 1.0 + nrm(ks[20], (DEPTH, D_MODEL), 0.02),
        'ln2_b': nrm(ks[21], (DEPTH, D_MODEL), 0.02),
    }


def reference(x, c, ctx, c_ctx, w_ada, b_ada, w_in, lam_q, lam_k, subln_g, conv_w, conv_b,
              lru_w_gates, lru_b_gates, lru_lambda, w_out, ln1_g, ln1_b, w_ffn_in, w_ffn_out,
              ln2_g, ln2_b):
    S = x.shape[1]
    cos, sin = axial_rope_tables(S)
    xc = ctx
    for l in range(DEPTH):
        last = l == DEPTH - 1
        lam_init = 0.8 - 0.6 * math.exp(-0.3 * l)
        mod = jax.nn.silu(c) @ w_ada[l] + b_ada[l]
        mod_c = jax.nn.silu(c_ctx) @ w_ada[l] + b_ada[l]
        sh1, sc1, g1, sh2, sc2, g2 = jnp.split(mod[:, None, :], N_MOD, axis=-1)
        csh1, csc1, cg1, csh2, csc2, cg2 = jnp.split(mod_c, N_MOD, axis=-1)
        y, yc = mixing_sublayer(modulate(layer_norm(x), sh1, sc1), modulate(layer_norm(xc), csh1, csc1),
                                cos, sin, w_in[l], lam_q[l], lam_k[l], lam_init, subln_g[l],
                                conv_w[l], conv_b[l], lru_w_gates[l], lru_b_gates[l], lru_lambda[l],
                                w_out[l], not last)
        x = layer_norm(DEEPNORM_ALPHA * x + g1 * y, ln1_g[l], ln1_b[l])
        f = swiglu(modulate(layer_norm(x), sh2, sc2), w_ffn_in[l], w_ffn_out[l])
        x = layer_norm(DEEPNORM_ALPHA * x + g2 * f, ln2_g[l], ln2_b[l])
        if not last:
            xc = layer_norm(DEEPNORM_ALPHA * xc + cg1 * yc, ln1_g[l], ln1_b[l])
            fc = swiglu(modulate(layer_norm(xc), csh2, csc2), w_ffn_in[l], w_ffn_out[l])
            xc = layer_norm(DEEPNORM_ALPHA * xc + cg2 * fc, ln2_g[l], ln2_b[l])
    return x
```

```python
import functools
import math

import jax
import jax.numpy as jnp
from jax import lax
from jax.experimental import pallas as pl
from jax.experimental.pallas import tpu as pltpu

F32 = jnp.float32
BF16 = jnp.bfloat16

GRID_W = 64
ATT_HEADS = 4
ATT_QK_DIM = 64
ATT_V_DIM = 128
ATT_WIDTH = ATT_HEADS * ATT_V_DIM
LRU_WIDTH = 512
LRU_BLOCKS = 8
LRU_BLOCK_DIM = LRU_WIDTH // LRU_BLOCKS
LRU_C = 8.0
CONV_W = 4
CONV_LEFT = CONV_W // 2
N_MOD = 6
ROPE_BASE = 10000.0
ROPE_FREQS = ATT_QK_DIM // 4
ATT_SCALE = ATT_QK_DIM ** -0.5
DEPTH = 1
DEEPNORM_ALPHA = (2 * DEPTH) ** 0.25
LN_EPS = 1e-6
RMS_EPS = 1e-5
LAM_INIT = 0.8 - 0.6 * math.exp(-0.3 * 0)
LOG2E = math.log2(math.e)

V7X_SUBLANES = 8
V7X_LANES = 128
V7X_VMEM_BYTES = 64 * 1024 * 1024
VMEM_LIMIT = V7X_VMEM_BYTES - 6 * 1024 * 1024

MOD_ROWS = 24
TM_INPROJ = 512
TQ_ATTN = 512
TK_ATTN = 512
TT_LRU = 256
TM_TAIL = 256
NEG_BIG = -1e30


def _layer_norm(x):
  mu = jnp.mean(x, axis=-1, keepdims=True)
  xc = x - mu
  var = jnp.mean(xc * xc, axis=-1, keepdims=True)
  return xc * lax.rsqrt(var + LN_EPS)


def _sigmoid(x):
  return 1.0 / (1.0 + jnp.exp(-x))


def _dot(a, b):
  return jnp.dot(a, b, preferred_element_type=F32)


def _mod_kernel(c_ref, w_ref, b_ref, o_ref):
  c = c_ref[...]
  s = c * _sigmoid(c)
  o_ref[...] = jnp.dot(s, w_ref[...], preferred_element_type=F32,
                       precision=lax.Precision.HIGHEST) + b_ref[...]


def _mod_call(cc, w_ada, b_ada):
  d = cc.shape[1]
  return pl.pallas_call(
      _mod_kernel,
      grid=(N_MOD,),
      in_specs=[
          pl.BlockSpec((MOD_ROWS, d), lambda j: (0, 0)),
          pl.BlockSpec((d, d), lambda j: (0, j)),
          pl.BlockSpec((1, d), lambda j: (0, j)),
      ],
      out_specs=pl.BlockSpec((MOD_ROWS, d), lambda j: (0, j)),
      out_shape=jax.ShapeDtypeStruct((MOD_ROWS, N_MOD * d), F32),
      name="adaln_mod",
  )(cc, w_ada, b_ada)


def _modulated_ln(x, mod_ref, shift_row):
  sh = mod_ref[0, shift_row:shift_row + 1, :]
  sc = mod_ref[0, shift_row + 1:shift_row + 2, :]
  return (_layer_norm(x) * (1.0 + sc) + sh).astype(BF16)


def _rope(t, cos, sa, sb):
  outs = []
  for h in range(ATT_HEADS):
    th = t[:, h * V7X_LANES:(h + 1) * V7X_LANES]
    up = pltpu.roll(th, V7X_LANES - ROPE_FREQS, 1)
    dn = pltpu.roll(th, ROPE_FREQS, 1)
    outs.append(th * cos + up * sa + dn * sb)
  return jnp.concatenate(outs, axis=1)


def _gelu_tanh(x):
  c = math.sqrt(2.0 / math.pi)
  return x * (0.5 * (1.0 + jnp.tanh(c * (x + 0.044715 * (x * x * x)))))


def _inproj_latent_kernel(x_ref, mod_ref, w_ref, cos_ref, sa_ref, sb_ref,
                          q_ref, k_ref, v_ref, u_ref, g_ref):
  h = _modulated_ln(x_ref[0], mod_ref, 0)
  w = ATT_WIDTH
  cos, sa, sb = cos_ref[...], sa_ref[...], sb_ref[...]
  q = _rope(_dot(h, w_ref[:, 0:w]), cos, sa, sb)
  q_ref[0] = (q * (ATT_SCALE * LOG2E)).astype(BF16)
  k_ref[0] = _rope(_dot(h, w_ref[:, w:2 * w]), cos, sa, sb).astype(BF16)
  v_ref[0] = _dot(h, w_ref[:, 2 * w:3 * w]).astype(BF16)
  u_ref[0] = _dot(h, w_ref[:, 3 * w:4 * w])
  g_ref[0] = _gelu_tanh(_dot(h, w_ref[:, 4 * w:5 * w])).astype(BF16)


def _inproj_context_kernel(x_ref, mod_ref, w_ref, k_ref, v_ref, u_ref):
  h = _modulated_ln(x_ref[0], mod_ref, 0)
  w = ATT_WIDTH
  k_ref[0] = _dot(h, w_ref[:, w:2 * w]).astype(BF16)
  v_ref[0] = _dot(h, w_ref[:, 2 * w:3 * w]).astype(BF16)
  u_ref[0] = _dot(h, w_ref[:, 3 * w:4 * w])


def _inproj_latent_call(x, mod3, w_in, cos, sa, sb):
  b, s, d = x.shape
  tm = TM_INPROJ
  n_in = w_in.shape[1]
  tok = lambda bi, i: (bi, i, 0)
  tab = lambda bi, i: (i, 0)
  out_bf = jax.ShapeDtypeStruct((b, s, ATT_WIDTH), BF16)
  return pl.pallas_call(
      _inproj_latent_kernel,
      grid=(b, s // tm),
      in_specs=[
          pl.BlockSpec((1, tm, d), tok),
          pl.BlockSpec((1, N_MOD, d), lambda bi, i: (bi, 0, 0)),
          pl.BlockSpec((d, n_in), lambda bi, i: (0, 0), pipeline_mode=pl.Buffered(1)),
          pl.BlockSpec((tm, V7X_LANES), tab),
          pl.BlockSpec((tm, V7X_LANES), tab),
          pl.BlockSpec((tm, V7X_LANES), tab),
      ],
      out_specs=[pl.BlockSpec((1, tm, ATT_WIDTH), tok)] * 5,
      out_shape=[out_bf, out_bf, out_bf,
                 jax.ShapeDtypeStruct((b, s, LRU_WIDTH), F32), out_bf],
      compiler_params=pltpu.CompilerParams(
          dimension_semantics=("parallel", "parallel"), vmem_limit_bytes=VMEM_LIMIT),
      name="inproj_latent",
  )(x, mod3, w_in, cos, sa, sb)


def _inproj_context_call(ctx, mod3, w_in, ctx_row):
  b, t, d = ctx.shape
  n_in = w_in.shape[1]
  tok = lambda bi: (bi, 0, 0)
  out_bf = jax.ShapeDtypeStruct((b, t, ATT_WIDTH), BF16)
  return pl.pallas_call(
      _inproj_context_kernel,
      grid=(b,),
      in_specs=[
          pl.BlockSpec((1, t, d), tok),
          pl.BlockSpec((1, N_MOD, d), lambda bi: (ctx_row, 0, 0)),
          pl.BlockSpec((d, n_in), lambda bi: (0, 0), pipeline_mode=pl.Buffered(1)),
      ],
      out_specs=[pl.BlockSpec((1, t, ATT_WIDTH), tok)] * 3,
      out_shape=[out_bf, out_bf, jax.ShapeDtypeStruct((b, t, LRU_WIDTH), F32)],
      compiler_params=pltpu.CompilerParams(
          dimension_semantics=("parallel",), vmem_limit_bytes=VMEM_LIMIT),
      name="inproj_context",
  )(ctx, mod3, w_in)


def _attn_kernel(lq_ref, lk_ref, g_ref, q_ref, k_ref, kc_ref, v_ref, vc_ref, o_ref,
                 m_ref, l_ref, acc_ref, *, n_chunks, tk):
  q = q_ref[0]
  lane = lax.broadcasted_iota(jnp.int32, q.shape, 1)
  zero = jnp.zeros_like(q)
  qs = (jnp.where(lane < ATT_QK_DIM, q, zero), jnp.where(lane >= ATT_QK_DIM, q, zero))
  m_ref[...] = jnp.full(m_ref.shape, NEG_BIG, F32)
  l_ref[...] = jnp.zeros(l_ref.shape, F32)
  acc_ref[...] = jnp.zeros(acc_ref.shape, F32)

  def process(kb, vb):
    for mi in range(2):
      s = lax.dot_general(qs[mi], kb, (((1,), (1,)), ((), ())), preferred_element_type=F32)
      m_old = m_ref[mi]
      m_new = jnp.maximum(m_old, jnp.max(s, axis=-1, keepdims=True))
      alpha = jnp.exp2(m_old - m_new)
      p = jnp.exp2(s - m_new)
      l_ref[mi] = alpha * l_ref[mi] + jnp.sum(p, axis=-1, keepdims=True)
      acc_ref[mi] = alpha * acc_ref[mi] + _dot(p.astype(BF16), vb)
      m_ref[mi] = m_new

  def body(i, carry):
    off = pl.multiple_of(i * tk, tk)
    process(k_ref[0, pl.ds(off, tk), :], v_ref[0, pl.ds(off, tk), :])
    return carry

  lax.fori_loop(0, n_chunks, body, 0)
  process(kc_ref[0], vc_ref[0])

  lq = jnp.sum(lq_ref[...] * lk_ref[...], axis=-1, keepdims=True)
  lam = jnp.exp(lq[0:1]) - jnp.exp(lq[1:2]) + LAM_INIT
  o = acc_ref[0] / l_ref[0] - lam * (acc_ref[1] / l_ref[1])
  ms = jnp.mean(o * o, axis=-1, keepdims=True)
  y = o * lax.rsqrt(ms + RMS_EPS) * g_ref[...] * (1.0 - LAM_INIT)
  o_ref[0] = y.astype(BF16)


def _attn_call(lam_q, lam_k, subln_g, q, k, kc, v, vc):
  b, s, _ = q.shape
  t = kc.shape[1]
  tq, tk = TQ_ATTN, TK_ATTN
  dv = ATT_V_DIM
  small = lambda bi, h, i: (0, 0)
  qmap = lambda bi, h, i: (bi, i, h)
  kvmap = lambda bi, h, i: (bi, 0, h)
  return pl.pallas_call(
      functools.partial(_attn_kernel, n_chunks=s // tk, tk=tk),
      grid=(b, ATT_HEADS, s // tq),
      in_specs=[
          pl.BlockSpec(lam_q.shape, small),
          pl.BlockSpec(lam_k.shape, small),
          pl.BlockSpec(subln_g.shape, small),
          pl.BlockSpec((1, tq, dv), qmap),
          pl.BlockSpec((1, s, dv), kvmap),
          pl.BlockSpec((1, t, dv), kvmap),
          pl.BlockSpec((1, s, dv), kvmap),
          pl.BlockSpec((1, t, dv), kvmap),
      ],
      out_specs=pl.BlockSpec((1, tq, dv), qmap),
      out_shape=jax.ShapeDtypeStruct((b, s, ATT_WIDTH), BF16),
      scratch_shapes=[
          pltpu.VMEM((2, tq, 1), F32),
          pltpu.VMEM((2, tq, 1), F32),
          pltpu.VMEM((2, tq, dv), F32),
      ],
      compiler_params=pltpu.CompilerParams(
          dimension_semantics=("parallel", "parallel", "arbitrary"),
          vmem_limit_bytes=VMEM_LIMIT),
      name="diff_attention",
  )(lam_q, lam_k, subln_g, q, k, kc, v, vc)


def _lru_coeffs(uw, wg, bg, sp, cw, cb, tt):
  h = V7X_SUBLANES
  uconv = cb
  for j in range(CONV_W):
    lo = h - CONV_LEFT + j
    uconv = uconv + uw[lo:lo + tt] * cw[j:j + 1, :]
  z = _dot(uconv.astype(BF16), wg) + bg
  r = _sigmoid(z[:, :LRU_WIDTH])
  gate_i = _sigmoid(z[:, LRU_WIDTH:])
  log_a = (-LRU_C) * r * sp
  a = jnp.exp(log_a)
  mult = jnp.sqrt(-jnp.tanh(log_a) * (a * a + 1.0))
  return a, mult * gate_i * uconv


def _scan_tile(a, b, carry, reverse, store):
  tt, w = a.shape
  h8 = V7X_SUBLANES
  n = tt // h8
  a3 = a.reshape(n, h8, w)
  b3 = b.reshape(n, h8, w)
  row = lax.broadcasted_iota(jnp.int32, (n, h8, w), 1)
  for k in (1, 2, 4):
    if reverse:
      keep, shift = row < h8 - k, h8 - k
    else:
      keep, shift = row >= k, k
    a_s = jnp.where(keep, pltpu.roll(a3, shift, 1), 1.0)
    b_s = jnp.where(keep, pltpu.roll(b3, shift, 1), 0.0)
    b3 = a3 * b_s + b3
    a3 = a3 * a_s
  c = jnp.broadcast_to(carry, (h8, w))
  for g in (range(n - 1, -1, -1) if reverse else range(n)):
    hg = a3[g] * c + b3[g]
    store(g, hg)
    edge = hg[0:1] if reverse else hg[h8 - 1:h8]
    c = jnp.broadcast_to(edge, (h8, w))
  return c[0:1]


def _softplus(x):
  return jnp.maximum(x, 0.0) + jnp.log1p(jnp.exp(-jnp.abs(x)))


def _lru_kernel(u_ref, uc_ref, gg_ref, wg_ref, bg_ref, lam_ref, cw_ref, cb_ref, y_ref,
                hf_ref, ht_ref, *, tt):
  s = u_ref.shape[1]
  w = LRU_WIDTH
  h8 = V7X_SUBLANES
  n_tiles = s // tt
  cw = cw_ref[...]
  cb = cb_ref[...]
  sp = _softplus(-lam_ref[...])
  zeros8 = jnp.zeros((h8, w), F32)

  def coeffs(window, d):
    return _lru_coeffs(window, wg_ref[d], bg_ref[d], sp[d:d + 1, :], cw, cb, tt)

  def latent_window(r0):
    lo = jnp.maximum(r0 - h8, 0)
    hi = jnp.minimum(r0 + tt, s - h8)
    prev = jnp.where(r0 > 0, u_ref[0, pl.ds(pl.multiple_of(lo, h8), h8), :], zeros8)
    nxt = jnp.where(r0 + tt < s, u_ref[0, pl.ds(pl.multiple_of(hi, h8), h8), :], zeros8)
    mid = u_ref[0, pl.ds(pl.multiple_of(r0, tt), tt), :]
    return jnp.concatenate([prev, mid, nxt], axis=0)

  def store_tile(g, hg):
    ht_ref[g * h8:(g + 1) * h8, :] = hg

  ctx_window = jnp.concatenate([zeros8, uc_ref[0], zeros8], axis=0)
  zero_state = jnp.zeros((1, w), F32)

  a, b = coeffs(ctx_window, 0)
  state = _scan_tile(a, b, zero_state, False, store_tile)

  def fwd_body(i, carry):
    r0 = i * tt
    a, b = coeffs(latent_window(r0), 0)
    carry = _scan_tile(a, b, carry, False, store_tile)
    hf_ref[pl.ds(pl.multiple_of(r0, tt), tt), :] = ht_ref[...]
    return carry

  lax.fori_loop(0, n_tiles, fwd_body, state)

  a, b = coeffs(ctx_window, 1)
  state = _scan_tile(a, b, zero_state, True, store_tile)

  def bwd_body(j, carry):
    r0 = (n_tiles - 1 - j) * tt
    a, b = coeffs(latent_window(r0), 1)
    carry = _scan_tile(a, b, carry, True, store_tile)
    rows = pl.ds(pl.multiple_of(r0, tt), tt)
    y = (hf_ref[rows, :] + ht_ref[...]) * gg_ref[0, rows, :].astype(F32)
    y_ref[0, rows, :] = y.astype(BF16)
    return carry

  lax.fori_loop(0, n_tiles, bwd_body, state)


def _lru_call(u, uc, gg, wg, bg, lam, cw, cb):
  b, s, w = u.shape
  t = uc.shape[1]
  tt = TT_LRU
  assert t == tt
  bmap = lambda bi: (bi, 0, 0)
  c2 = lambda bi: (0, 0)
  c3 = lambda bi: (0, 0, 0)
  return pl.pallas_call(
      functools.partial(_lru_kernel, tt=tt),
      grid=(b,),
      in_specs=[
          pl.BlockSpec((1, s, w), bmap),
          pl.BlockSpec((1, t, w), bmap),
          pl.BlockSpec((1, s, w), bmap),
          pl.BlockSpec(wg.shape, c3, pipeline_mode=pl.Buffered(1)),
          pl.BlockSpec(bg.shape, c3),
          pl.BlockSpec(lam.shape, c2),
          pl.BlockSpec(cw.shape, c2),
          pl.BlockSpec(cb.shape, c2),
      ],
      out_specs=pl.BlockSpec((1, s, w), bmap),
      out_shape=jax.ShapeDtypeStruct((b, s, w), BF16),
      scratch_shapes=[pltpu.VMEM((s, w), F32), pltpu.VMEM((tt, w), F32)],
      compiler_params=pltpu.CompilerParams(
          dimension_semantics=("parallel",), vmem_limit_bytes=VMEM_LIMIT),
      name="rglru",
  )(u, uc, gg, wg, bg, lam, cw, cb)


def _tail_kernel(o_ref, y_ref, x_ref, mod_ref, wo_ref, w1_ref, w2_ref, ln_ref, out_ref):
  d = x_ref.shape[2]
  f = w2_ref.shape[0]
  mix = _dot(o_ref[0], wo_ref[0:ATT_WIDTH, :]) + _dot(y_ref[0], wo_ref[ATT_WIDTH:, :])
  g1 = mod_ref[0, 2:3, :]
  g2 = mod_ref[0, 5:6, :]
  x1 = _layer_norm(DEEPNORM_ALPHA * x_ref[0] + g1 * mix) * ln_ref[0:1, :] + ln_ref[1:2, :]
  h2 = _modulated_ln(x1, mod_ref, 3)
  a = _dot(h2, w1_ref[:, 0:f])
  b = _dot(h2, w1_ref[:, f:2 * f])
  act = (a * _sigmoid(a) * b).astype(BF16)
  ff = _dot(act, w2_ref[...])
  x2 = _layer_norm(DEEPNORM_ALPHA * x1 + g2 * ff) * ln_ref[2:3, :] + ln_ref[3:4, :]
  out_ref[0] = x2.astype(out_ref.dtype)
  del d


def _tail_call(o, y, x, mod3, wo, w1, w2, ln):
  b, s, d = x.shape
  tm = TM_TAIL
  tok = lambda bi, i: (bi, i, 0)
  c2 = lambda bi, i: (0, 0)
  return pl.pallas_call(
      _tail_kernel,
      grid=(b, s // tm),
      in_specs=[
          pl.BlockSpec((1, tm, ATT_WIDTH), tok),
          pl.BlockSpec((1, tm, LRU_WIDTH), tok),
          pl.BlockSpec((1, tm, d), tok),
          pl.BlockSpec((1, N_MOD, d), lambda bi, i: (bi, 0, 0)),
          pl.BlockSpec(wo.shape, c2, pipeline_mode=pl.Buffered(1)),
          pl.BlockSpec(w1.shape, c2, pipeline_mode=pl.Buffered(1)),
          pl.BlockSpec(w2.shape, c2, pipeline_mode=pl.Buffered(1)),
          pl.BlockSpec(ln.shape, c2),
      ],
      out_specs=pl.BlockSpec((1, tm, d), tok),
      out_shape=jax.ShapeDtypeStruct((b, s, d), x.dtype),
      compiler_params=pltpu.CompilerParams(
          dimension_semantics=("parallel", "parallel"), vmem_limit_bytes=VMEM_LIMIT),
      name="outproj_ffn",
  )(o, y, x, mod3, wo, w1, w2, ln)


def _rope_tables(s):
  rows = s // GRID_W
  row = jnp.repeat(jnp.arange(rows, dtype=F32), GRID_W)
  col = jnp.tile(jnp.arange(GRID_W, dtype=F32), rows)
  inv_freq = ROPE_BASE ** (-jnp.arange(ROPE_FREQS, dtype=F32) / ROPE_FREQS)
  ar, ac = row[:, None] * inv_freq, col[:, None] * inv_freq
  cr, sr, cc, sc = jnp.cos(ar), jnp.sin(ar), jnp.cos(ac), jnp.sin(ac)
  z = jnp.zeros_like(sr)
  reps = V7X_LANES // ATT_QK_DIM
  cos = jnp.tile(jnp.concatenate([cr, cr, cc, cc], axis=1), (1, reps))
  sa = jnp.tile(jnp.concatenate([-sr, z, -sc, z], axis=1), (1, reps))
  sb = jnp.tile(jnp.concatenate([z, sr, z, sc], axis=1), (1, reps))
  return cos, sa, sb


def _gate_weights(w_gates, b_gates):
  eye = jnp.eye(LRU_BLOCKS, dtype=w_gates.dtype)
  dense = jnp.einsum("dghij,hk->dghikj", w_gates, eye).reshape(2, 2, LRU_WIDTH, LRU_WIDTH)
  wg = jnp.concatenate([dense[:, 0], dense[:, 1]], axis=-1).astype(BF16)
  bg = b_gates.reshape(2, 1, 2 * LRU_WIDTH)
  return wg, bg


def kernel(x, c, ctx, c_ctx, w_ada, b_ada, w_in, lam_q, lam_k, subln_g, conv_w, conv_b,
           lru_w_gates, lru_b_gates, lru_lambda, w_out, ln1_g, ln1_b, w_ffn_in, w_ffn_out,
           ln2_g, ln2_b):
  b, s, d = x.shape
  assert w_ada.shape[0] == DEPTH and b + 1 <= MOD_ROWS
  cc = jnp.zeros((MOD_ROWS, d), F32).at[:b].set(c).at[b].set(c_ctx)
  mod3 = _mod_call(cc, w_ada[0], b_ada[0][None, :]).reshape(MOD_ROWS, N_MOD, d)

  w_in_b = w_in[0].astype(BF16)
  cos, sa, sb = _rope_tables(s)
  q, k, v, u, gg = _inproj_latent_call(x, mod3, w_in_b, cos, sa, sb)
  kc, vc, uc = _inproj_context_call(ctx, mod3, w_in_b, b)

  o = _attn_call(lam_q[0], lam_k[0], subln_g[0][None, :], q, k, kc, v, vc)

  wg, bg = _gate_weights(lru_w_gates[0], lru_b_gates[0])
  y = _lru_call(u, uc, gg, wg, bg, lru_lambda[0], conv_w[0], conv_b[0][None, :])

  ln = jnp.stack([ln1_g[0], ln1_b[0], ln2_g[0], ln2_b[0]], axis=0)
  return _tail_call(o, y, x, mod3, w_out[0].astype(BF16), w_ffn_in[0].astype(BF16),
                    w_ffn_out[0].astype(BF16), ln)
```

```python
import functools
import math

import jax
import jax.numpy as jnp
from jax import lax
from jax.experimental import pallas as pl
from jax.experimental.pallas import tpu as pltpu

F32 = jnp.float32
BF16 = jnp.bfloat16

GRID_W = 64
ATT_HEADS = 4
ATT_QK_DIM = 64
ATT_V_DIM = 128
ATT_WIDTH = ATT_HEADS * ATT_V_DIM
LRU_WIDTH = 512
LRU_BLOCKS = 8
LRU_BLOCK_DIM = LRU_WIDTH // LRU_BLOCKS
LRU_C = 8.0
CONV_W = 4
CONV_LEFT = CONV_W // 2
N_MOD = 6
ROPE_BASE = 10000.0
ROPE_FREQS = ATT_QK_DIM // 4
ATT_SCALE = ATT_QK_DIM ** -0.5
DEPTH = 1
DEEPNORM_ALPHA = (2 * DEPTH) ** 0.25
LN_EPS = 1e-6
RMS_EPS = 1e-5
LAM_INIT = 0.8 - 0.6 * math.exp(-0.3 * 0)
LOG2E = math.log2(math.e)
LN2 = math.log(2.0)

V7X_SUBLANES = 8
V7X_LANES = 128
V7X_VMEM_BYTES = 64 * 1024 * 1024
VMEM_LIMIT = V7X_VMEM_BYTES - 6 * 1024 * 1024

MOD_ROWS = 24
TM_INPROJ = 512
INPROJ_ROWS = 256
TQ_ATTN = 512
TK_ATTN = 512
ATTN_SUB = 256
ATTN_QCOLS = 256
ATTN_FLAGS = None
BF16_SUBLANES = 16
VT_ROWS = ATT_V_DIM + BF16_SUBLANES
TT_LRU = 256
TM_TAIL = 512
TAIL_ROWS = 256
NEG_BIG = -1e30


def _layer_norm(x):
  mu = jnp.mean(x, axis=-1, keepdims=True)
  xc = x - mu
  var = jnp.mean(xc * xc, axis=-1, keepdims=True)
  return xc * lax.rsqrt(var + LN_EPS)


def _sigmoid(x):
  return 1.0 / (1.0 + jnp.exp(-x))


def _dot(a, b):
  return jnp.dot(a, b, preferred_element_type=F32)


def _mod_kernel(c_ref, w_ref, b_ref, o_ref):
  c = c_ref[...]
  s = c * _sigmoid(c)
  o_ref[...] = jnp.dot(s, w_ref[...], preferred_element_type=F32,
                       precision=lax.Precision.HIGHEST) + b_ref[...]


def _mod_call(cc, w_ada, b_ada):
  d = cc.shape[1]
  return pl.pallas_call(
      _mod_kernel,
      grid=(N_MOD,),
      in_specs=[
          pl.BlockSpec((MOD_ROWS, d), lambda j: (0, 0)),
          pl.BlockSpec((d, d), lambda j: (0, j)),
          pl.BlockSpec((1, d), lambda j: (0, j)),
      ],
      out_specs=pl.BlockSpec((MOD_ROWS, d), lambda j: (0, j)),
      out_shape=jax.ShapeDtypeStruct((MOD_ROWS, N_MOD * d), F32),
      name="adaln_mod",
  )(cc, w_ada, b_ada)


def _modulated_ln(x, mod_ref, shift_row):
  sh = mod_ref[0, shift_row:shift_row + 1, :]
  sc = mod_ref[0, shift_row + 1:shift_row + 2, :]
  return (_layer_norm(x) * (1.0 + sc) + sh).astype(BF16)


def _rope(t, cos, sa, sb):
  outs = []
  for h in range(ATT_HEADS):
    th = t[:, h * V7X_LANES:(h + 1) * V7X_LANES]
    up = pltpu.roll(th, V7X_LANES - ROPE_FREQS, 1)
    dn = pltpu.roll(th, ROPE_FREQS, 1)
    outs.append(th * cos + up * sa + dn * sb)
  return jnp.concatenate(outs, axis=1)


def _gelu_tanh(x):
  c = math.sqrt(2.0 / math.pi)
  return x * (0.5 * (1.0 + jnp.tanh(c * (x + 0.044715 * (x * x * x)))))


def _dot_nt(a, b):
  return lax.dot_general(a, b, (((1,), (1,)), ((), ())), preferred_element_type=F32)


def _values_t(wvt_ref, vone_ref, h):
  return (_dot_nt(wvt_ref[...], h) + vone_ref[...]).astype(BF16)


def _inproj_latent_kernel(x_ref, mod_ref, w_ref, wvt_ref, vone_ref, cos_ref, sa_ref, sb_ref,
                          q_ref, k_ref, vt_ref, u_ref, g_ref):
  w = ATT_WIDTH
  tm = x_ref.shape[1]
  groups = [slice(r, r + INPROJ_ROWS) for r in range(0, tm, INPROJ_ROWS)]
  hs = [_modulated_ln(x_ref[0, g, :], mod_ref, 0) for g in groups]
  for g, h in zip(groups, hs):
    q = _rope(_dot(h, w_ref[:, 0:w]), cos_ref[g, :], sa_ref[g, :], sb_ref[g, :])
    q_ref[0, g, :] = (q * (ATT_SCALE * LOG2E)).astype(BF16)
  for g, h in zip(groups, hs):
    k = _rope(_dot(h, w_ref[:, w:2 * w]), cos_ref[g, :], sa_ref[g, :], sb_ref[g, :])
    k_ref[0, g, :] = k.astype(BF16)
  for g, h in zip(groups, hs):
    vt_ref[0, :, g] = _values_t(wvt_ref, vone_ref, h)
  for g, h in zip(groups, hs):
    u_ref[0, g, :] = _dot(h, w_ref[:, 3 * w:4 * w])
  for g, h in zip(groups, hs):
    g_ref[0, g, :] = _gelu_tanh(_dot(h, w_ref[:, 4 * w:5 * w])).astype(BF16)


def _inproj_context_kernel(x_ref, mod_ref, w_ref, wvt_ref, vone_ref, k_ref, vt_ref, u_ref):
  h = _modulated_ln(x_ref[0], mod_ref, 0)
  w = ATT_WIDTH
  k_ref[0] = _dot(h, w_ref[:, w:2 * w]).astype(BF16)
  vt_ref[0] = _values_t(wvt_ref, vone_ref, h)
  u_ref[0] = _dot(h, w_ref[:, 3 * w:4 * w])


def _inproj_latent_call(x, mod3, w_in, wvt, vone, cos, sa, sb):
  b, s, d = x.shape
  tm = TM_INPROJ
  n_in = w_in.shape[1]
  tok = lambda bi, i: (bi, i, 0)
  tab = lambda bi, i: (i, 0)
  const = lambda bi, i: (0, 0)
  out_bf = jax.ShapeDtypeStruct((b, s, ATT_WIDTH), BF16)
  out_t = jax.ShapeDtypeStruct((b, wvt.shape[0], s), BF16)
  tok_spec = pl.BlockSpec((1, tm, ATT_WIDTH), tok)
  return pl.pallas_call(
      _inproj_latent_kernel,
      grid=(b, s // tm),
      in_specs=[
          pl.BlockSpec((1, tm, d), tok),
          pl.BlockSpec((1, N_MOD, d), lambda bi, i: (bi, 0, 0)),
          pl.BlockSpec((d, n_in), const, pipeline_mode=pl.Buffered(1)),
          pl.BlockSpec(wvt.shape, const, pipeline_mode=pl.Buffered(1)),
          pl.BlockSpec(vone.shape, const),
          pl.BlockSpec((tm, V7X_LANES), tab),
          pl.BlockSpec((tm, V7X_LANES), tab),
          pl.BlockSpec((tm, V7X_LANES), tab),
      ],
      out_specs=[tok_spec, tok_spec,
                 pl.BlockSpec((1, wvt.shape[0], tm), lambda bi, i: (bi, 0, i)),
                 tok_spec, tok_spec],
      out_shape=[out_bf, out_bf, out_t,
                 jax.ShapeDtypeStruct((b, s, LRU_WIDTH), F32), out_bf],
      compiler_params=pltpu.CompilerParams(
          dimension_semantics=("parallel", "parallel"), vmem_limit_bytes=VMEM_LIMIT),
      name="inproj_latent",
  )(x, mod3, w_in, wvt, vone, cos, sa, sb)


def _inproj_context_call(ctx, mod3, w_in, wvt, vone, ctx_row):
  b, t, d = ctx.shape
  n_in = w_in.shape[1]
  tok = lambda bi: (bi, 0, 0)
  const = lambda bi: (0, 0)
  out_bf = jax.ShapeDtypeStruct((b, t, ATT_WIDTH), BF16)
  out_t = jax.ShapeDtypeStruct((b, wvt.shape[0], t), BF16)
  return pl.pallas_call(
      _inproj_context_kernel,
      grid=(b,),
      in_specs=[
          pl.BlockSpec((1, t, d), tok),
          pl.BlockSpec((1, N_MOD, d), lambda bi: (ctx_row, 0, 0)),
          pl.BlockSpec((d, n_in), const, pipeline_mode=pl.Buffered(1)),
          pl.BlockSpec(wvt.shape, const, pipeline_mode=pl.Buffered(1)),
          pl.BlockSpec(vone.shape, const),
      ],
      out_specs=[pl.BlockSpec((1, t, ATT_WIDTH), tok), pl.BlockSpec((1, wvt.shape[0], t), tok),
                 pl.BlockSpec((1, t, LRU_WIDTH), tok)],
      out_shape=[out_bf, out_t, jax.ShapeDtypeStruct((b, t, LRU_WIDTH), F32)],
      compiler_params=pltpu.CompilerParams(
          dimension_semantics=("parallel",), vmem_limit_bytes=VMEM_LIMIT),
      name="inproj_context",
  )(ctx, mod3, w_in, wvt, vone)


def _attn_kernel(lq_ref, lk_ref, g_ref, q_ref, k_ref, kc_ref, vt_ref, vct_ref, o_ref, *, n_chunks, tk):
  tq = q_ref.shape[1]
  sub = ATTN_SUB
  q = q_ref[0]
  lane = lax.broadcasted_iota(jnp.int32, q.shape, 1)
  zero = jnp.zeros_like(q)
  qs = (jnp.where(lane < ATT_QK_DIM, q, zero), jnp.where(lane >= ATT_QK_DIM, q, zero))

  nq = ATTN_QCOLS
  n_col = tq // nq
  chunks = [(k_ref[0, c * tk:(c + 1) * tk, :], vt_ref[0, :, c * tk:(c + 1) * tk]) for c in range(n_chunks)]
  chunks.append((kc_ref[0], vct_ref[0]))

  def score_piece(kb, r0, mi):
    st = _dot_nt(kb[r0:r0 + sub], qs[mi])
    return st, jnp.max(st, axis=0, keepdims=True)

  def pv_piece(tiles, m_cur, alpha, acc_old, vtb, ci):
    cols = slice(ci * nq, (ci + 1) * nq)
    p = jnp.concatenate([jnp.exp2(st[:, cols] - m_cur[:, cols]).astype(BF16) for st in tiles], axis=0)
    return alpha[:, cols] * acc_old + _dot(vtb, p)

  dv = ATT_V_DIM
  m = [jnp.full((1, tq), NEG_BIG, F32)] * 2
  acc = [[jnp.zeros((vt_ref.shape[1], nq), F32)] * n_col for _ in range(2)]

  def a_list(c):
    kb = chunks[c][0]
    return [(kb, r0, mi) for r0 in range(0, kb.shape[0], sub) for mi in range(2)]

  tiles, cms = [[], []], [None, None]
  for kb, r0, mi in a_list(0):
    st, cm = score_piece(kb, r0, mi)
    tiles[mi].append(st)
    cms[mi] = cm if cms[mi] is None else jnp.maximum(cms[mi], cm)
  for c in range(len(chunks)):
    m_new = [jnp.maximum(m[mi], cms[mi]) for mi in range(2)]
    alpha = [jnp.exp2(m[mi] - m_new[mi]) for mi in range(2)]
    cur_tiles, tiles, cms = tiles, [[], []], [None, None]
    a_pieces = a_list(c + 1) if c + 1 < len(chunks) else []
    bc_pieces = [(mi, ci) for mi in range(2) for ci in range(n_col)]
    for i in range(max(len(a_pieces), len(bc_pieces))):
      if i < len(a_pieces):
        kb, r0, mi = a_pieces[i]
        st, cm = score_piece(kb, r0, mi)
        tiles[mi].append(st)
        cms[mi] = cm if cms[mi] is None else jnp.maximum(cms[mi], cm)
      if i < len(bc_pieces):
        mi, ci = bc_pieces[i]
        acc[mi][ci] = pv_piece(cur_tiles[mi], m_new[mi], alpha[mi], acc[mi][ci], chunks[c][1], ci)
    m = m_new

  lq = jnp.sum(lq_ref[...] * lk_ref[...], axis=-1, keepdims=True)
  lam = jnp.exp(lq[0:1]) - jnp.exp(lq[1:2]) + LAM_INIT
  full = [jnp.concatenate(acc[mi], axis=1) for mi in range(2)]
  ot = (full[0][:dv] / full[0][dv:dv + 1]
        - lam * (full[1][:dv] / full[1][dv:dv + 1]))
  o = ot.T
  ms = jnp.mean(o * o, axis=-1, keepdims=True)
  y = o * lax.rsqrt(ms + RMS_EPS) * g_ref[...] * (1.0 - LAM_INIT)
  o_ref[0] = y.astype(BF16)


def _attn_call(lam_q, lam_k, subln_g, q, k, kc, vt, vct):
  b, s, _ = q.shape
  t = kc.shape[1]
  tq, tk = TQ_ATTN, TK_ATTN
  assert tk == 2 * ATTN_SUB and t == ATTN_SUB and s // tk >= 2 and (s // tk) % 2 == 0
  dv = ATT_V_DIM
  small = lambda bi, h, i: (0, 0)
  qmap = lambda bi, h, i: (bi, i, h)
  kmap = lambda bi, h, i: (bi, 0, h)
  vtmap = lambda bi, h, i: (bi, h, 0)
  return pl.pallas_call(
      functools.partial(_attn_kernel, n_chunks=s // tk, tk=tk),
      grid=(b, ATT_HEADS, s // tq),
      in_specs=[
          pl.BlockSpec(lam_q.shape, small),
          pl.BlockSpec(lam_k.shape, small),
          pl.BlockSpec(subln_g.shape, small),
          pl.BlockSpec((1, tq, dv), qmap),
          pl.BlockSpec((1, s, dv), kmap),
          pl.BlockSpec((1, t, dv), kmap),
          pl.BlockSpec((1, VT_ROWS, s), vtmap),
          pl.BlockSpec((1, VT_ROWS, t), vtmap),
      ],
      out_specs=pl.BlockSpec((1, tq, dv), qmap),
      out_shape=jax.ShapeDtypeStruct((b, s, ATT_WIDTH), BF16),
      compiler_params=pltpu.CompilerParams(
          dimension_semantics=("parallel", "parallel", "arbitrary"),
          vmem_limit_bytes=VMEM_LIMIT, flags=ATTN_FLAGS),
      name="diff_attention",
  )(lam_q, lam_k, subln_g, q, k, kc, vt, vct)


def _centred_conv(uw, cw, cb, tt):
  assert CONV_W == 4 and CONV_LEFT == 2
  h8 = V7X_SUBLANES
  w = uw.shape[1]
  n = tt // h8
  grp = uw.reshape(n + 2, h8, w)
  row = lax.broadcasted_iota(jnp.int32, (n, h8, w), 1)
  lo, hi = grp[0:n + 1], grp[1:n + 2]
  r1 = pltpu.roll(lo, 1, 1)
  r2 = pltpu.roll(lo, 2, 1)
  rp = pltpu.roll(hi, h8 - 1, 1)
  xm1 = jnp.where(row >= 1, r1[1:], r1[:-1])
  xm2 = jnp.where(row >= 2, r2[1:], r2[:-1])
  xp1 = jnp.where(row <= h8 - 2, rp[:-1], rp[1:])
  out = cb + xm2 * cw[0:1, :] + xm1 * cw[1:2, :] + grp[1:n + 1] * cw[2:3, :] + xp1 * cw[3:4, :]
  return out.reshape(tt, w)


def _lru_coeffs(uw, wg, bg, sp2, cw, cb, tt):
  uconv = _centred_conv(uw, cw, cb, tt)
  z = _dot(uconv.astype(BF16), wg) + bg
  gates = 1.0 / (1.0 + jnp.exp2(z))
  r = gates[:, :LRU_WIDTH]
  gate_i = gates[:, LRU_WIDTH:]
  la2 = r * sp2
  a = jnp.exp2(la2)
  x = jnp.tanh(la2 * (-LN2)) * (a * a + 1.0)
  mult = jnp.where(x > 0.0, x * lax.rsqrt(x), 0.0)
  return a, mult * gate_i * uconv


def _scan_tile(a, b, carry, reverse, store):
  tt, w = a.shape
  h8 = V7X_SUBLANES
  n = tt // h8
  a3 = a.reshape(n, h8, w)
  b3 = b.reshape(n, h8, w)
  row = lax.broadcasted_iota(jnp.int32, (n, h8, w), 1)
  for k in (1, 2, 4):
    if reverse:
      keep, shift = row < h8 - k, h8 - k
    else:
      keep, shift = row >= k, k
    a_s = jnp.where(keep, pltpu.roll(a3, shift, 1), 1.0)
    b_s = jnp.where(keep, pltpu.roll(b3, shift, 1), 0.0)
    b3 = a3 * b_s + b3
    a3 = a3 * a_s
  c = jnp.broadcast_to(carry, (h8, w))
  for g in (range(n - 1, -1, -1) if reverse else range(n)):
    hg = a3[g] * c + b3[g]
    store(g, hg)
    edge = hg[0:1] if reverse else hg[h8 - 1:h8]
    c = jnp.broadcast_to(edge, (h8, w))
  return c[0:1]


def _softplus(x):
  return jnp.maximum(x, 0.0) + jnp.log1p(jnp.exp(-jnp.abs(x)))


def _lru_kernel(u_ref, uc_ref, gg_ref, wg_ref, bg_ref, lam_ref, cw_ref, cb_ref, y_ref,
                hf_ref, ht_ref, *, tt):
  s = u_ref.shape[1]
  w = LRU_WIDTH
  h8 = V7X_SUBLANES
  n_tiles = s // tt
  cw = cw_ref[...]
  cb = cb_ref[...]
  sp2 = (-LRU_C * LOG2E) * _softplus(-lam_ref[...])
  zeros8 = jnp.zeros((h8, w), F32)

  def coeffs(window, d):
    return _lru_coeffs(window, wg_ref[d], bg_ref[d], sp2[d:d + 1, :], cw, cb, tt)

  def latent_window(r0):
    lo = jnp.maximum(r0 - h8, 0)
    hi = jnp.minimum(r0 + tt, s - h8)
    prev = jnp.where(r0 > 0, u_ref[0, pl.ds(pl.multiple_of(lo, h8), h8), :], zeros8)
    nxt = jnp.where(r0 + tt < s, u_ref[0, pl.ds(pl.multiple_of(hi, h8), h8), :], zeros8)
    mid = u_ref[0, pl.ds(pl.multiple_of(r0, tt), tt), :]
    return jnp.concatenate([prev, mid, nxt], axis=0)

  def store_tile(g, hg):
    ht_ref[g * h8:(g + 1) * h8, :] = hg

  ctx_window = jnp.concatenate([zeros8, uc_ref[0], zeros8], axis=0)
  zero_state = jnp.zeros((1, w), F32)

  a, b = coeffs(ctx_window, 0)
  state = _scan_tile(a, b, zero_state, False, store_tile)

  def fwd_body(i, carry):
    r0 = i * tt
    a, b = coeffs(latent_window(r0), 0)
    carry = _scan_tile(a, b, carry, False, store_tile)
    hf_ref[pl.ds(pl.multiple_of(r0, tt), tt), :] = ht_ref[...]
    return carry

  lax.fori_loop(0, n_tiles, fwd_body, state)

  a, b = coeffs(ctx_window, 1)
  state = _scan_tile(a, b, zero_state, True, store_tile)

  def bwd_body(j, carry):
    r0 = (n_tiles - 1 - j) * tt
    a, b = coeffs(latent_window(r0), 1)
    carry = _scan_tile(a, b, carry, True, store_tile)
    rows = pl.ds(pl.multiple_of(r0, tt), tt)
    y = (hf_ref[rows, :] + ht_ref[...]) * gg_ref[0, rows, :].astype(F32)
    y_ref[0, rows, :] = y.astype(BF16)
    return carry

  lax.fori_loop(0, n_tiles, bwd_body, state)


def _lru_call(u, uc, gg, wg, bg, lam, cw, cb):
  b, s, w = u.shape
  t = uc.shape[1]
  tt = TT_LRU
  assert t == tt
  bmap = lambda bi: (bi, 0, 0)
  c2 = lambda bi: (0, 0)
  c3 = lambda bi: (0, 0, 0)
  return pl.pallas_call(
      functools.partial(_lru_kernel, tt=tt),
      grid=(b,),
      in_specs=[
          pl.BlockSpec((1, s, w), bmap),
          pl.BlockSpec((1, t, w), bmap),
          pl.BlockSpec((1, s, w), bmap),
          pl.BlockSpec(wg.shape, c3, pipeline_mode=pl.Buffered(1)),
          pl.BlockSpec(bg.shape, c3),
          pl.BlockSpec(lam.shape, c2),
          pl.BlockSpec(cw.shape, c2),
          pl.BlockSpec(cb.shape, c2),
      ],
      out_specs=pl.BlockSpec((1, s, w), bmap),
      out_shape=jax.ShapeDtypeStruct((b, s, w), BF16),
      scratch_shapes=[pltpu.VMEM((s, w), F32), pltpu.VMEM((tt, w), F32)],
      compiler_params=pltpu.CompilerParams(
          dimension_semantics=("parallel",), vmem_limit_bytes=VMEM_LIMIT),
      name="rglru",
  )(u, uc, gg, wg, bg, lam, cw, cb)


def _tail_kernel(o_ref, y_ref, x_ref, mod_ref, wo_ref, w1_ref, w2_ref, ln_ref, out_ref):
  f = w2_ref.shape[0]
  tm = x_ref.shape[1]
  g1 = mod_ref[0, 2:3, :]
  g2 = mod_ref[0, 5:6, :]
  groups = [slice(r, r + TAIL_ROWS) for r in range(0, tm, TAIL_ROWS)]
  mix = [_dot(o_ref[0, g, :], wo_ref[0:ATT_WIDTH, :]) + _dot(y_ref[0, g, :], wo_ref[ATT_WIDTH:, :])
         for g in groups]
  x1 = [_layer_norm(DEEPNORM_ALPHA * x_ref[0, g, :] + g1 * mx) * ln_ref[0:1, :] + ln_ref[1:2, :]
        for g, mx in zip(groups, mix)]
  h2 = [_modulated_ln(v, mod_ref, 3) for v in x1]
  ab = [(_dot(h, w1_ref[:, 0:f]), _dot(h, w1_ref[:, f:2 * f])) for h in h2]
  act = [(a * _sigmoid(a) * b).astype(BF16) for a, b in ab]
  ff = [_dot(t, w2_ref[...]) for t in act]
  for g, v, t in zip(groups, x1, ff):
    x2 = _layer_norm(DEEPNORM_ALPHA * v + g2 * t) * ln_ref[2:3, :] + ln_ref[3:4, :]
    out_ref[0, g, :] = x2.astype(out_ref.dtype)


def _tail_call(o, y, x, mod3, wo, w1, w2, ln):
  b, s, d = x.shape
  tm = TM_TAIL
  tok = lambda bi, i: (bi, i, 0)
  c2 = lambda bi, i: (0, 0)
  return pl.pallas_call(
      _tail_kernel,
      grid=(b, s // tm),
      in_specs=[
          pl.BlockSpec((1, tm, ATT_WIDTH), tok),
          pl.BlockSpec((1, tm, LRU_WIDTH), tok),
          pl.BlockSpec((1, tm, d), tok),
          pl.BlockSpec((1, N_MOD, d), lambda bi, i: (bi, 0, 0)),
          pl.BlockSpec(wo.shape, c2, pipeline_mode=pl.Buffered(1)),
          pl.BlockSpec(w1.shape, c2, pipeline_mode=pl.Buffered(1)),
          pl.BlockSpec(w2.shape, c2, pipeline_mode=pl.Buffered(1)),
          pl.BlockSpec(ln.shape, c2),
      ],
      out_specs=pl.BlockSpec((1, tm, d), tok),
      out_shape=jax.ShapeDtypeStruct((b, s, d), x.dtype),
      compiler_params=pltpu.CompilerParams(
          dimension_semantics=("parallel", "parallel"), vmem_limit_bytes=VMEM_LIMIT),
      name="outproj_ffn",
  )(o, y, x, mod3, wo, w1, w2, ln)


def _rope_tables(s):
  rows = s // GRID_W
  row = jnp.repeat(jnp.arange(rows, dtype=F32), GRID_W)
  col = jnp.tile(jnp.arange(GRID_W, dtype=F32), rows)
  inv_freq = ROPE_BASE ** (-jnp.arange(ROPE_FREQS, dtype=F32) / ROPE_FREQS)
  ar, ac = row[:, None] * inv_freq, col[:, None] * inv_freq
  cr, sr, cc, sc = jnp.cos(ar), jnp.sin(ar), jnp.cos(ac), jnp.sin(ac)
  z = jnp.zeros_like(sr)
  reps = V7X_LANES // ATT_QK_DIM
  cos = jnp.tile(jnp.concatenate([cr, cr, cc, cc], axis=1), (1, reps))
  sa = jnp.tile(jnp.concatenate([-sr, z, -sc, z], axis=1), (1, reps))
  sb = jnp.tile(jnp.concatenate([z, sr, z, sc], axis=1), (1, reps))
  return cos, sa, sb


def _gate_weights(w_gates, b_gates):
  eye = jnp.eye(LRU_BLOCKS, dtype=w_gates.dtype)
  dense = jnp.einsum("dghij,hk->dghikj", w_gates * (-LOG2E), eye).reshape(2, 2, LRU_WIDTH, LRU_WIDTH)
  wg = jnp.concatenate([dense[:, 0], dense[:, 1]], axis=-1).astype(BF16)
  bg = (b_gates * (-LOG2E)).reshape(2, 1, 2 * LRU_WIDTH)
  return wg, bg


def kernel(x, c, ctx, c_ctx, w_ada, b_ada, w_in, lam_q, lam_k, subln_g, conv_w, conv_b,
           lru_w_gates, lru_b_gates, lru_lambda, w_out, ln1_g, ln1_b, w_ffn_in, w_ffn_out,
           ln2_g, ln2_b):
  b, s, d = x.shape
  assert w_ada.shape[0] == DEPTH and b + 1 <= MOD_ROWS
  cc = jnp.zeros((MOD_ROWS, d), F32).at[:b].set(c).at[b].set(c_ctx)
  mod3 = _mod_call(cc, w_ada[0], b_ada[0][None, :]).reshape(MOD_ROWS, N_MOD, d)

  w_in_b = w_in[0].astype(BF16)
  wv_heads = w_in_b[:, 2 * ATT_WIDTH:3 * ATT_WIDTH].T.reshape(ATT_HEADS, ATT_V_DIM, d)
  wvt = jnp.pad(wv_heads, ((0, 0), (0, VT_ROWS - ATT_V_DIM), (0, 0))).reshape(ATT_HEADS * VT_ROWS, d)
  vone = jnp.zeros((ATT_HEADS, VT_ROWS, 1), F32).at[:, ATT_V_DIM].set(1.0).reshape(-1, 1)
  cos, sa, sb = _rope_tables(s)
  q, k, vt, u, gg = _inproj_latent_call(x, mod3, w_in_b, wvt, vone, cos, sa, sb)
  kc, vct, uc = _inproj_context_call(ctx, mod3, w_in_b, wvt, vone, b)

  o = _attn_call(lam_q[0], lam_k[0], subln_g[0][None, :], q, k, kc, vt, vct)

  wg, bg = _gate_weights(lru_w_gates[0], lru_b_gates[0])
  y = _lru_call(u, uc, gg, wg, bg, lru_lambda[0], conv_w[0], conv_b[0][None, :])

  ln = jnp.stack([ln1_g[0], ln1_b[0], ln2_g[0], ln2_b[0]], axis=0)
  return _tail_call(o, y, x, mod3, w_out[0].astype(BF16), w_ffn_in[0].astype(BF16),
                    w_ffn_out[0].astype(BF16), ln)
```

```python
import functools
import math

import jax
import jax.numpy as jnp
from jax import lax
from jax.experimental import pallas as pl
from jax.experimental.pallas import tpu as pltpu

F32 = jnp.float32
BF16 = jnp.bfloat16

GRID_W = 64
ATT_HEADS = 4
ATT_QK_DIM = 64
ATT_V_DIM = 128
ATT_WIDTH = ATT_HEADS * ATT_V_DIM
LRU_WIDTH = 512
LRU_BLOCKS = 8
LRU_BLOCK_DIM = LRU_WIDTH // LRU_BLOCKS
LRU_C = 8.0
CONV_W = 4
CONV_LEFT = CONV_W // 2
N_MOD = 6
ROPE_BASE = 10000.0
ROPE_FREQS = ATT_QK_DIM // 4
ATT_SCALE = ATT_QK_DIM ** -0.5
DEPTH = 1
DEEPNORM_ALPHA = (2 * DEPTH) ** 0.25
LN_EPS = 1e-6
RMS_EPS = 1e-5
LAM_INIT = 0.8 - 0.6 * math.exp(-0.3 * 0)
LOG2E = math.log2(math.e)
LN2 = math.log(2.0)

V7X_SUBLANES = 8
V7X_LANES = 128
V7X_VMEM_BYTES = 64 * 1024 * 1024
VMEM_LIMIT = V7X_VMEM_BYTES - 6 * 1024 * 1024

MOD_ROWS = 24
TM_INPROJ = 512
INPROJ_ROWS = 256
TQ_ATTN = 512
TK_ATTN = 512
ATTN_SUB = 256
ATTN_QCOLS = 256
ATTN_FLAGS = None
BF16_SUBLANES = 16
VT_ROWS = ATT_V_DIM + BF16_SUBLANES
TT_LRU = 256
SCAN_SEG = TT_LRU // V7X_SUBLANES + 1
TM_TAIL = 512
TAIL_ROWS = 256
NEG_BIG = -1e30


def _layer_norm(x):
  mu = jnp.mean(x, axis=-1, keepdims=True)
  xc = x - mu
  var = jnp.mean(xc * xc, axis=-1, keepdims=True)
  return xc * lax.rsqrt(var + LN_EPS)


def _sigmoid(x):
  return 1.0 / (1.0 + jnp.exp(-x))


def _dot(a, b):
  return jnp.dot(a, b, preferred_element_type=F32)


def _mod_kernel(c_ref, w_ref, b_ref, o_ref):
  c = c_ref[...]
  s = c * _sigmoid(c)
  o_ref[...] = jnp.dot(s, w_ref[...], preferred_element_type=F32,
                       precision=lax.Precision.HIGHEST) + b_ref[...]


def _mod_call(cc, w_ada, b_ada):
  d = cc.shape[1]
  return pl.pallas_call(
      _mod_kernel,
      grid=(N_MOD,),
      in_specs=[
          pl.BlockSpec((MOD_ROWS, d), lambda j: (0, 0)),
          pl.BlockSpec((d, d), lambda j: (0, j)),
          pl.BlockSpec((1, d), lambda j: (0, j)),
      ],
      out_specs=pl.BlockSpec((MOD_ROWS, d), lambda j: (0, j)),
      out_shape=jax.ShapeDtypeStruct((MOD_ROWS, N_MOD * d), F32),
      name="adaln_mod",
  )(cc, w_ada, b_ada)


def _modulated_ln(x, mod_ref, shift_row):
  sh = mod_ref[0, shift_row:shift_row + 1, :]
  sc = mod_ref[0, shift_row + 1:shift_row + 2, :]
  return (_layer_norm(x) * (1.0 + sc) + sh).astype(BF16)


def _rope(t, cos, sa, sb):
  outs = []
  for h in range(ATT_HEADS):
    th = t[:, h * V7X_LANES:(h + 1) * V7X_LANES]
    up = pltpu.roll(th, V7X_LANES - ROPE_FREQS, 1)
    dn = pltpu.roll(th, ROPE_FREQS, 1)
    outs.append(th * cos + up * sa + dn * sb)
  return jnp.concatenate(outs, axis=1)


def _gelu_tanh(x):
  c = math.sqrt(2.0 / math.pi)
  return x * (0.5 * (1.0 + jnp.tanh(c * (x + 0.044715 * (x * x * x)))))


def _dot_nt(a, b):
  return lax.dot_general(a, b, (((1,), (1,)), ((), ())), preferred_element_type=F32)


def _values_t(wvt_ref, vone_ref, h):
  return (_dot_nt(wvt_ref[...], h) + vone_ref[...]).astype(BF16)


def _inproj_latent_kernel(x_ref, mod_ref, w_ref, wvt_ref, vone_ref, cos_ref, sa_ref, sb_ref,
                          q_ref, k_ref, vt_ref, u_ref, g_ref):
  w = ATT_WIDTH
  tm = x_ref.shape[1]
  groups = [slice(r, r + INPROJ_ROWS) for r in range(0, tm, INPROJ_ROWS)]
  hs = [_modulated_ln(x_ref[0, g, :], mod_ref, 0) for g in groups]
  for g, h in zip(groups, hs):
    q = _rope(_dot(h, w_ref[:, 0:w]), cos_ref[g, :], sa_ref[g, :], sb_ref[g, :])
    q_ref[0, g, :] = (q * (ATT_SCALE * LOG2E)).astype(BF16)
  for g, h in zip(groups, hs):
    k = _rope(_dot(h, w_ref[:, w:2 * w]), cos_ref[g, :], sa_ref[g, :], sb_ref[g, :])
    k_ref[0, g, :] = k.astype(BF16)
  for g, h in zip(groups, hs):
    vt_ref[0, :, g] = _values_t(wvt_ref, vone_ref, h)
  for g, h in zip(groups, hs):
    u_ref[0, g, :] = _dot(h, w_ref[:, 3 * w:4 * w])
  for g, h in zip(groups, hs):
    g_ref[0, g, :] = _gelu_tanh(_dot(h, w_ref[:, 4 * w:5 * w])).astype(BF16)


def _inproj_context_kernel(x_ref, mod_ref, w_ref, wvt_ref, vone_ref, k_ref, vt_ref, u_ref):
  h = _modulated_ln(x_ref[0], mod_ref, 0)
  w = ATT_WIDTH
  k_ref[0] = _dot(h, w_ref[:, w:2 * w]).astype(BF16)
  vt_ref[0] = _values_t(wvt_ref, vone_ref, h)
  u_ref[0] = _dot(h, w_ref[:, 3 * w:4 * w])


def _inproj_latent_call(x, mod3, w_in, wvt, vone, cos, sa, sb):
  b, s, d = x.shape
  tm = TM_INPROJ
  n_in = w_in.shape[1]
  tok = lambda bi, i: (bi, i, 0)
  tab = lambda bi, i: (i, 0)
  const = lambda bi, i: (0, 0)
  out_bf = jax.ShapeDtypeStruct((b, s, ATT_WIDTH), BF16)
  out_t = jax.ShapeDtypeStruct((b, wvt.shape[0], s), BF16)
  tok_spec = pl.BlockSpec((1, tm, ATT_WIDTH), tok)
  return pl.pallas_call(
      _inproj_latent_kernel,
      grid=(b, s // tm),
      in_specs=[
          pl.BlockSpec((1, tm, d), tok),
          pl.BlockSpec((1, N_MOD, d), lambda bi, i: (bi, 0, 0)),
          pl.BlockSpec((d, n_in), const, pipeline_mode=pl.Buffered(1)),
          pl.BlockSpec(wvt.shape, const, pipeline_mode=pl.Buffered(1)),
          pl.BlockSpec(vone.shape, const),
          pl.BlockSpec((tm, V7X_LANES), tab),
          pl.BlockSpec((tm, V7X_LANES), tab),
          pl.BlockSpec((tm, V7X_LANES), tab),
      ],
      out_specs=[tok_spec, tok_spec,
                 pl.BlockSpec((1, wvt.shape[0], tm), lambda bi, i: (bi, 0, i)),
                 tok_spec, tok_spec],
      out_shape=[out_bf, out_bf, out_t,
                 jax.ShapeDtypeStruct((b, s, LRU_WIDTH), F32), out_bf],
      compiler_params=pltpu.CompilerParams(
          dimension_semantics=("parallel", "parallel"), vmem_limit_bytes=VMEM_LIMIT),
      name="inproj_latent",
  )(x, mod3, w_in, wvt, vone, cos, sa, sb)


def _inproj_context_call(ctx, mod3, w_in, wvt, vone, ctx_row):
  b, t, d = ctx.shape
  n_in = w_in.shape[1]
  tok = lambda bi: (bi, 0, 0)
  const = lambda bi: (0, 0)
  out_bf = jax.ShapeDtypeStruct((b, t, ATT_WIDTH), BF16)
  out_t = jax.ShapeDtypeStruct((b, wvt.shape[0], t), BF16)
  return pl.pallas_call(
      _inproj_context_kernel,
      grid=(b,),
      in_specs=[
          pl.BlockSpec((1, t, d), tok),
          pl.BlockSpec((1, N_MOD, d), lambda bi: (ctx_row, 0, 0)),
          pl.BlockSpec((d, n_in), const, pipeline_mode=pl.Buffered(1)),
          pl.BlockSpec(wvt.shape, const, pipeline_mode=pl.Buffered(1)),
          pl.BlockSpec(vone.shape, const),
      ],
      out_specs=[pl.BlockSpec((1, t, ATT_WIDTH), tok), pl.BlockSpec((1, wvt.shape[0], t), tok),
                 pl.BlockSpec((1, t, LRU_WIDTH), tok)],
      out_shape=[out_bf, out_t, jax.ShapeDtypeStruct((b, t, LRU_WIDTH), F32)],
      compiler_params=pltpu.CompilerParams(
          dimension_semantics=("parallel",), vmem_limit_bytes=VMEM_LIMIT),
      name="inproj_context",
  )(ctx, mod3, w_in, wvt, vone)


def _attn_kernel(lq_ref, lk_ref, g_ref, q_ref, k_ref, kc_ref, vt_ref, vct_ref, o_ref,
                 s0_ref, cm0_ref, acc_ref, *, n_chunks, tk, tq):
  sub = ATTN_SUB
  n_sub = tk // sub
  nq = ATTN_QCOLS
  n_col = tq // nq
  n_q = q_ref.shape[1] // tq
  dv = ATT_V_DIM
  n_all = n_chunks + 1

  def keys(c, r0):
    return kc_ref[0, r0:r0 + sub, :] if c == n_chunks else k_ref[0, c * tk + r0:c * tk + r0 + sub, :]

  def n_keys(c):
    return kc_ref.shape[1] if c == n_chunks else tk

  def values_t(c):
    return vct_ref[0] if c == n_chunks else vt_ref[0, :, c * tk:(c + 1) * tk]

  def masked_q(i):
    q = q_ref[0, pl.ds(pl.multiple_of(i * tq, tq), tq), :]
    lane = lax.broadcasted_iota(jnp.int32, q.shape, 1)
    zero = jnp.zeros_like(q)
    return (jnp.where(lane < ATT_QK_DIM, q, zero), jnp.where(lane >= ATT_QK_DIM, q, zero))

  def score_piece(c, r0, qm):
    st = _dot_nt(keys(c, r0), qm)
    return st, jnp.max(st, axis=0, keepdims=True)

  def prob_piece(tiles, m_cur, ci):
    cols = slice(ci * nq, (ci + 1) * nq)
    return jnp.concatenate([jnp.exp2(st[:, cols] - m_cur[:, cols]).astype(BF16) for st in tiles], axis=0)

  def a_list(c, qs):
    return [(c, r0, mi, qs[mi]) for r0 in range(0, n_keys(c), sub) for mi in range(2)]

  def run_pieces(a_pieces, b_fn, c_fn, n_bc):
    tiles, cms = [[], []], [None, None]
    for i in range(max(len(a_pieces), n_bc)):
      if i < len(a_pieces):
        c, r0, mi, qm = a_pieces[i]
        st, cm = score_piece(c, r0, qm)
        tiles[mi].append(st)
        cms[mi] = cm if cms[mi] is None else jnp.maximum(cms[mi], cm)
      if i < n_bc:
        b_fn(i)
        c_fn(i)
    return tiles, cms

  def hand_over(tiles, cms):
    for mi in range(2):
      for j in range(n_sub):
        s0_ref[mi, j * sub:(j + 1) * sub, :] = tiles[mi][j]
      cm0_ref[mi] = cms[mi]

  hand_over(*run_pieces(a_list(0, masked_q(0)), None, None, 0))

  def q_tile(i, carry):
    qs = masked_q(i)
    qs_next = masked_q(jnp.minimum(i + 1, n_q - 1))
    tiles = [[s0_ref[mi, j * sub:(j + 1) * sub, :] for j in range(n_sub)] for mi in range(2)]
    cms = [cm0_ref[mi] for mi in range(2)]
    m = [jnp.full((1, tq), NEG_BIG, F32)] * 2
    acc = [[jnp.zeros((vt_ref.shape[1], nq), F32)] * n_col for _ in range(2)]
    for c in range(n_all):
      m_new = [jnp.maximum(m[mi], cms[mi]) for mi in range(2)]
      alpha = [jnp.exp2(m[mi] - m_new[mi]) for mi in range(2)]
      cur_tiles = tiles
      a_pieces = a_list(c + 1, qs) if c + 1 < n_all else a_list(0, qs_next)

      probs = {}

      def b_fn(idx, cur_tiles=cur_tiles, m_new=m_new, probs=probs):
        mi, ci = divmod(idx, n_col)
        probs[idx] = prob_piece(cur_tiles[mi], m_new[mi], ci)

      def c_fn(idx, c=c, alpha=alpha, probs=probs):
        mi, ci = divmod(idx, n_col)
        cols = slice(ci * nq, (ci + 1) * nq)
        acc[mi][ci] = alpha[mi][:, cols] * acc[mi][ci] + _dot(values_t(c), probs[idx])

      tiles, cms = run_pieces(a_pieces, b_fn, c_fn, 2 * n_col)
      m = m_new
      if c == 0:
        finish(jnp.maximum(i - 1, 0))
    hand_over(tiles, cms)
    for mi in range(2):
      for ci in range(n_col):
        acc_ref[mi, :, ci * nq:(ci + 1) * nq] = acc[mi][ci]
    return carry

  def finish(j):
    lq = jnp.sum(lq_ref[...] * lk_ref[...], axis=-1, keepdims=True)
    lam = jnp.exp(lq[0:1]) - jnp.exp(lq[1:2]) + LAM_INIT
    ot = (acc_ref[0, :dv, :] / acc_ref[0, dv:dv + 1, :]
          - lam * (acc_ref[1, :dv, :] / acc_ref[1, dv:dv + 1, :]))
    o = ot.T
    ms = jnp.mean(o * o, axis=-1, keepdims=True)
    y = o * lax.rsqrt(ms + RMS_EPS) * g_ref[...] * (1.0 - LAM_INIT)
    o_ref[0, pl.ds(pl.multiple_of(j * tq, tq), tq), :] = y.astype(BF16)

  acc_ref[...] = jnp.ones(acc_ref.shape, F32)
  lax.fori_loop(0, n_q, q_tile, 0)
  finish(n_q - 1)


def _attn_call(lam_q, lam_k, subln_g, q, k, kc, vt, vct):
  b, s, _ = q.shape
  t = kc.shape[1]
  tq, tk = TQ_ATTN, TK_ATTN
  assert tk == 2 * ATTN_SUB and t == ATTN_SUB and s % tk == 0 and s % tq == 0
  dv = ATT_V_DIM
  small = lambda bi, h: (0, 0)
  qmap = lambda bi, h: (bi, 0, h)
  vtmap = lambda bi, h: (bi, h, 0)
  return pl.pallas_call(
      functools.partial(_attn_kernel, n_chunks=s // tk, tk=tk, tq=tq),
      grid=(b, ATT_HEADS),
      in_specs=[
          pl.BlockSpec(lam_q.shape, small),
          pl.BlockSpec(lam_k.shape, small),
          pl.BlockSpec(subln_g.shape, small),
          pl.BlockSpec((1, s, dv), qmap),
          pl.BlockSpec((1, s, dv), qmap),
          pl.BlockSpec((1, t, dv), qmap),
          pl.BlockSpec((1, VT_ROWS, s), vtmap),
          pl.BlockSpec((1, VT_ROWS, t), vtmap),
      ],
      out_specs=pl.BlockSpec((1, s, dv), qmap),
      out_shape=jax.ShapeDtypeStruct((b, s, ATT_WIDTH), BF16),
      scratch_shapes=[pltpu.VMEM((2, tk, tq), F32), pltpu.VMEM((2, 1, tq), F32),
                      pltpu.VMEM((2, VT_ROWS, tq), F32)],
      compiler_params=pltpu.CompilerParams(
          dimension_semantics=("parallel", "parallel"),
          vmem_limit_bytes=VMEM_LIMIT, flags=ATTN_FLAGS),
      name="diff_attention",
  )(lam_q, lam_k, subln_g, q, k, kc, vt, vct)


def _centred_conv(uw, cw, cb, tt):
  assert CONV_W == 4 and CONV_LEFT == 2
  h8 = V7X_SUBLANES
  w = uw.shape[1]
  n = tt // h8
  grp = uw.reshape(n + 2, h8, w)
  row = lax.broadcasted_iota(jnp.int32, (n, h8, w), 1)
  lo, hi = grp[0:n + 1], grp[1:n + 2]
  r1 = pltpu.roll(lo, 1, 1)
  r2 = pltpu.roll(lo, 2, 1)
  rp = pltpu.roll(hi, h8 - 1, 1)
  xm1 = jnp.where(row >= 1, r1[1:], r1[:-1])
  xm2 = jnp.where(row >= 2, r2[1:], r2[:-1])
  xp1 = jnp.where(row <= h8 - 2, rp[:-1], rp[1:])
  out = cb + xm2 * cw[0:1, :] + xm1 * cw[1:2, :] + grp[1:n + 1] * cw[2:3, :] + xp1 * cw[3:4, :]
  return out.reshape(tt, w)


def _lru_coeffs(uw, wg, bg, sp2, cw, cb, tt):
  uconv = _centred_conv(uw, cw, cb, tt)
  z = _dot(uconv.astype(BF16), wg) + bg
  gates = 0.5 * jnp.tanh(z) + 0.5
  r = gates[:, :LRU_WIDTH]
  gate_i = gates[:, LRU_WIDTH:]
  la2 = r * sp2
  a = jnp.exp2(la2)
  x = jnp.tanh(la2 * (-LN2)) * (a * a + 1.0)
  mult = jnp.where(x > 0.0, x * lax.rsqrt(x), 0.0)
  return a, mult * gate_i * uconv


def _scan_tile(a, b, carry, reverse, ca_ref, cb_ref):
  tt, w = a.shape
  h8, seg, lanes = V7X_SUBLANES, SCAN_SEG, V7X_LANES
  n_slab = w // lanes
  rows = h8 * seg
  assert rows >= tt and ca_ref.shape == (n_slab, rows, lanes)
  for j in range(n_slab):
    cols = slice(j * lanes, (j + 1) * lanes)
    ca_ref[j, 0:tt, :] = a[:, cols]
    cb_ref[j, 0:tt, :] = b[:, cols]
    ca_ref[j, tt:rows, :] = jnp.ones((rows - tt, lanes), F32)
    cb_ref[j, tt:rows, :] = jnp.zeros((rows - tt, lanes), F32)
  steps = range(seg - 1, -1, -1) if reverse else range(seg)
  order = range(h8 - 1, -1, -1) if reverse else range(h8)

  def at(tau):
    return pl.ds(tau, h8, stride=seg)

  outs, carry_out = [], []
  for j in range(n_slab):
    hl = jnp.zeros((h8, lanes), F32)
    ac = jnp.ones((h8, lanes), F32)
    for tau in steps:
      a_t = ca_ref[j, at(tau), :]
      hl = a_t * hl + cb_ref[j, at(tau), :]
      ac = ac * a_t
      cb_ref[j, at(tau), :] = hl
      ca_ref[j, at(tau), :] = ac
    c = carry[:, j * lanes:(j + 1) * lanes]
    cin = [None] * h8
    for s in order:
      cin[s] = c
      c = ac[s:s + 1] * c + hl[s:s + 1]
    carry_out.append(c)
    cin = jnp.concatenate(cin, axis=0)
    for tau in steps:
      cb_ref[j, at(tau), :] = cb_ref[j, at(tau), :] + ca_ref[j, at(tau), :] * cin
    outs.append(cb_ref[j, 0:tt, :])
  return jnp.concatenate(outs, axis=1), jnp.concatenate(carry_out, axis=1)


def _softplus(x):
  return jnp.maximum(x, 0.0) + jnp.log1p(jnp.exp(-jnp.abs(x)))


def _lru_kernel(u_ref, uc_ref, gg_ref, wg_ref, bg_ref, lam_ref, cw_ref, cb_ref, y_ref,
                hf_ref, sca_ref, scb_ref, *, tt):
  s = u_ref.shape[1]
  w = LRU_WIDTH
  h8 = V7X_SUBLANES
  n_tiles = s // tt
  cw = cw_ref[...]
  cb = cb_ref[...]
  sp2 = (-LRU_C * LOG2E) * _softplus(-lam_ref[...])
  zeros8 = jnp.zeros((h8, w), F32)

  def coeffs(window, d):
    return _lru_coeffs(window, wg_ref[d], bg_ref[d], sp2[d:d + 1, :], cw, cb, tt)

  def latent_window(r0):
    lo = jnp.maximum(r0 - h8, 0)
    hi = jnp.minimum(r0 + tt, s - h8)
    prev = jnp.where(r0 > 0, u_ref[0, pl.ds(pl.multiple_of(lo, h8), h8), :], zeros8)
    nxt = jnp.where(r0 + tt < s, u_ref[0, pl.ds(pl.multiple_of(hi, h8), h8), :], zeros8)
    mid = u_ref[0, pl.ds(pl.multiple_of(r0, tt), tt), :]
    return jnp.concatenate([prev, mid, nxt], axis=0)

  ctx_window = jnp.concatenate([zeros8, uc_ref[0], zeros8], axis=0)
  zero_state = jnp.zeros((1, w), F32)

  a, b = coeffs(ctx_window, 0)
  _, state = _scan_tile(a, b, zero_state, False, sca_ref, scb_ref)

  def fwd_body(i, carry):
    r0s = [(2 * i + t) * tt for t in range(2)]
    ab = [coeffs(latent_window(r0), 0) for r0 in r0s]
    for r0, (a, b) in zip(r0s, ab):
      h, carry = _scan_tile(a, b, carry, False, sca_ref, scb_ref)
      hf_ref[pl.ds(pl.multiple_of(r0, tt), tt), :] = h
    return carry

  lax.fori_loop(0, n_tiles // 2, fwd_body, state)

  a, b = coeffs(ctx_window, 1)
  _, state = _scan_tile(a, b, zero_state, True, sca_ref, scb_ref)

  def bwd_body(j, carry):
    r0s = [(n_tiles - 1 - 2 * j - t) * tt for t in range(2)]
    ab = [coeffs(latent_window(r0), 1) for r0 in r0s]
    for r0, (a, b) in zip(r0s, ab):
      h, carry = _scan_tile(a, b, carry, True, sca_ref, scb_ref)
      rows = pl.ds(pl.multiple_of(r0, tt), tt)
      y = (hf_ref[rows, :] + h) * gg_ref[0, rows, :].astype(F32)
      y_ref[0, rows, :] = y.astype(BF16)
    return carry

  lax.fori_loop(0, n_tiles // 2, bwd_body, state)


def _lru_call(u, uc, gg, wg, bg, lam, cw, cb):
  b, s, w = u.shape
  t = uc.shape[1]
  tt = TT_LRU
  assert t == tt
  bmap = lambda bi: (bi, 0, 0)
  c2 = lambda bi: (0, 0)
  c3 = lambda bi: (0, 0, 0)
  return pl.pallas_call(
      functools.partial(_lru_kernel, tt=tt),
      grid=(b,),
      in_specs=[
          pl.BlockSpec((1, s, w), bmap),
          pl.BlockSpec((1, t, w), bmap),
          pl.BlockSpec((1, s, w), bmap),
          pl.BlockSpec(wg.shape, c3, pipeline_mode=pl.Buffered(1)),
          pl.BlockSpec(bg.shape, c3),
          pl.BlockSpec(lam.shape, c2),
          pl.BlockSpec(cw.shape, c2),
          pl.BlockSpec(cb.shape, c2),
      ],
      out_specs=pl.BlockSpec((1, s, w), bmap),
      out_shape=jax.ShapeDtypeStruct((b, s, w), BF16),
      scratch_shapes=[pltpu.VMEM((s, w), F32)] + [
          pltpu.VMEM((w // V7X_LANES, V7X_SUBLANES * SCAN_SEG, V7X_LANES), F32)] * 2,
      compiler_params=pltpu.CompilerParams(
          dimension_semantics=("parallel",), vmem_limit_bytes=VMEM_LIMIT),
      name="rglru",
  )(u, uc, gg, wg, bg, lam, cw, cb)


def _tail_kernel(o_ref, y_ref, x_ref, mod_ref, wo_ref, w1_ref, w2_ref, ln_ref, out_ref):
  f = w2_ref.shape[0]
  tm = x_ref.shape[1]
  g1 = mod_ref[0, 2:3, :]
  g2 = mod_ref[0, 5:6, :]
  groups = [slice(r, r + TAIL_ROWS) for r in range(0, tm, TAIL_ROWS)]
  mix = [_dot(o_ref[0, g, :], wo_ref[0:ATT_WIDTH, :]) + _dot(y_ref[0, g, :], wo_ref[ATT_WIDTH:, :])
         for g in groups]
  x1 = [_layer_norm(DEEPNORM_ALPHA * x_ref[0, g, :] + g1 * mx) * ln_ref[0:1, :] + ln_ref[1:2, :]
        for g, mx in zip(groups, mix)]
  h2 = [_modulated_ln(v, mod_ref, 3) for v in x1]
  ab = [(_dot(h, w1_ref[:, 0:f]), _dot(h, w1_ref[:, f:2 * f])) for h in h2]
  act = [(a * _sigmoid(a) * b).astype(BF16) for a, b in ab]
  ff = [_dot(t, w2_ref[...]) for t in act]
  for g, v, t in zip(groups, x1, ff):
    x2 = _layer_norm(DEEPNORM_ALPHA * v + g2 * t) * ln_ref[2:3, :] + ln_ref[3:4, :]
    out_ref[0, g, :] = x2.astype(out_ref.dtype)


def _tail_call(o, y, x, mod3, wo, w1, w2, ln):
  b, s, d = x.shape
  tm = TM_TAIL
  tok = lambda bi, i: (bi, i, 0)
  c2 = lambda bi, i: (0, 0)
  return pl.pallas_call(
      _tail_kernel,
      grid=(b, s // tm),
      in_specs=[
          pl.BlockSpec((1, tm, ATT_WIDTH), tok),
          pl.BlockSpec((1, tm, LRU_WIDTH), tok),
          pl.BlockSpec((1, tm, d), tok),
          pl.BlockSpec((1, N_MOD, d), lambda bi, i: (bi, 0, 0)),
          pl.BlockSpec(wo.shape, c2, pipeline_mode=pl.Buffered(1)),
          pl.BlockSpec(w1.shape, c2, pipeline_mode=pl.Buffered(1)),
          pl.BlockSpec(w2.shape, c2, pipeline_mode=pl.Buffered(1)),
          pl.BlockSpec(ln.shape, c2),
      ],
      out_specs=pl.BlockSpec((1, tm, d), tok),
      out_shape=jax.ShapeDtypeStruct((b, s, d), x.dtype),
      compiler_params=pltpu.CompilerParams(
          dimension_semantics=("parallel", "parallel"), vmem_limit_bytes=VMEM_LIMIT),
      name="outproj_ffn",
  )(o, y, x, mod3, wo, w1, w2, ln)


def _rope_tables(s):
  rows = s // GRID_W
  row = jnp.repeat(jnp.arange(rows, dtype=F32), GRID_W)
  col = jnp.tile(jnp.arange(GRID_W, dtype=F32), rows)
  inv_freq = ROPE_BASE ** (-jnp.arange(ROPE_FREQS, dtype=F32) / ROPE_FREQS)
  ar, ac = row[:, None] * inv_freq, col[:, None] * inv_freq
  cr, sr, cc, sc = jnp.cos(ar), jnp.sin(ar), jnp.cos(ac), jnp.sin(ac)
  z = jnp.zeros_like(sr)
  reps = V7X_LANES // ATT_QK_DIM
  cos = jnp.tile(jnp.concatenate([cr, cr, cc, cc], axis=1), (1, reps))
  sa = jnp.tile(jnp.concatenate([-sr, z, -sc, z], axis=1), (1, reps))
  sb = jnp.tile(jnp.concatenate([z, sr, z, sc], axis=1), (1, reps))
  return cos, sa, sb


def _gate_weights(w_gates, b_gates):
  eye = jnp.eye(LRU_BLOCKS, dtype=w_gates.dtype)
  dense = jnp.einsum("dghij,hk->dghikj", w_gates * 0.5, eye).reshape(2, 2, LRU_WIDTH, LRU_WIDTH)
  wg = jnp.concatenate([dense[:, 0], dense[:, 1]], axis=-1).astype(BF16)
  bg = (b_gates * 0.5).reshape(2, 1, 2 * LRU_WIDTH)
  return wg, bg


def kernel(x, c, ctx, c_ctx, w_ada, b_ada, w_in, lam_q, lam_k, subln_g, conv_w, conv_b,
           lru_w_gates, lru_b_gates, lru_lambda, w_out, ln1_g, ln1_b, w_ffn_in, w_ffn_out,
           ln2_g, ln2_b):
  b, s, d = x.shape
  assert w_ada.shape[0] == DEPTH and b + 1 <= MOD_ROWS
  cc = jnp.zeros((MOD_ROWS, d), F32).at[:b].set(c).at[b].set(c_ctx)
  mod3 = _mod_call(cc, w_ada[0], b_ada[0][None, :]).reshape(MOD_ROWS, N_MOD, d)

  w_in_b = w_in[0].astype(BF16)
  wv_heads = w_in_b[:, 2 * ATT_WIDTH:3 * ATT_WIDTH].T.reshape(ATT_HEADS, ATT_V_DIM, d)
  wvt = jnp.pad(wv_heads, ((0, 0), (0, VT_ROWS - ATT_V_DIM), (0, 0))).reshape(ATT_HEADS * VT_ROWS, d)
  vone = jnp.zeros((ATT_HEADS, VT_ROWS, 1), F32).at[:, ATT_V_DIM].set(1.0).reshape(-1, 1)
  cos, sa, sb = _rope_tables(s)
  q, k, vt, u, gg = _inproj_latent_call(x, mod3, w_in_b, wvt, vone, cos, sa, sb)
  kc, vct, uc = _inproj_context_call(ctx, mod3, w_in_b, wvt, vone, b)

  o = _attn_call(lam_q[0], lam_k[0], subln_g[0][None, :], q, k, kc, vt, vct)

  wg, bg = _gate_weights(lru_w_gates[0], lru_b_gates[0])
  y = _lru_call(u, uc, gg, wg, bg, lru_lambda[0], conv_w[0], conv_b[0][None, :])

  ln = jnp.stack([ln1_g[0], ln1_b[0], ln2_g[0], ln2_b[0]], axis=0)
  return _tail_call(o, y, x, mod3, w_out[0].astype(BF16), w_ffn_in[0].astype(BF16),
                    w_ffn_out[0].astype(BF16), ln)
```
